```python
import jax, jax.numpy as jnp
from jax import lax
import numpy as np

D_MODEL = 1024
BATCH = 1
SEQ = 16384
DEPTH = 2

GRID_W = 64
CTX_LEN = 256
N_MIXERS = 2
N_HEADS = 16
HEAD_DIM = D_MODEL // N_HEADS
NA_MAX_ROWS = 8
NA_COLS = 16
POOL_SIZES = (2, 4, 8, 16)
N_POOL_GROUPS = len(POOL_SIZES)
POOL_GROUP_DIM = D_MODEL // N_POOL_GROUPS
N_EXPERTS = 32
TOP_K = 4
D_EXPERT = D_MODEL
SWIGLU_LIMIT = 7.0
SWIGLU_ALPHA = 1.702
RMS_EPS = 1e-6
N_ATTN_LAYERS = (DEPTH + N_MIXERS - 1) // N_MIXERS
N_POOL_LAYERS = DEPTH // N_MIXERS

kernel_name = 'hybrid_natten_pool_moe_dit'


def rms_norm(x, g):
    xf = x.astype(jnp.float32)
    y = xf * lax.rsqrt(jnp.mean(xf * xf, axis=-1, keepdims=True) + RMS_EPS)
    return (y * g.astype(jnp.float32)).astype(x.dtype)


def adaln(cond, w, b):
    mod = jax.nn.silu(cond) @ w + b
    return jnp.split(mod, 6, axis=-1)


def modulate(x, g, shift, scale):
    return rms_norm(x, g) * (1 + scale) + shift


def neighbourhood_attention(q, k, v, k_ctx, v_ctx, rpb):
    B, L, H, Dh = q.shape
    rows = L // GRID_W
    kh = min(NA_MAX_ROWS, rows)
    kw = NA_COLS
    q = q * (Dh ** -0.5)
    qg = q.reshape(B, rows, GRID_W, H, Dh).transpose(1, 0, 2, 3, 4)
    kg = k.reshape(B, rows, GRID_W, H, Dh)
    vg = v.reshape(B, rows, GRID_W, H, Dh)
    r_idx = jnp.arange(rows)
    row_start = jnp.clip(r_idx - kh // 2, 0, rows - kh)
    c_idx = jnp.arange(GRID_W)
    col_start = jnp.clip(c_idx - kw // 2, 0, GRID_W - kw)
    col_win = col_start[:, None] + jnp.arange(kw)[None, :]
    dc = col_win - c_idx[:, None] + (NA_COLS - 1)

    def row_block(args):
        q_r, r, r0 = args
        k_blk = lax.dynamic_slice_in_dim(kg, r0, kh, axis=1)
        v_blk = lax.dynamic_slice_in_dim(vg, r0, kh, axis=1)
        k_win = k_blk[:, :, col_win]
        v_win = v_blk[:, :, col_win]
        dr = r0 + jnp.arange(kh) - r + (NA_MAX_ROWS - 1)
        bias = rpb[:, dr[:, None, None], dc[None, :, :]]
        s_loc = jnp.einsum('bchd,bicjhd->bhcij', q_r, k_win, preferred_element_type=jnp.float32)
        s_loc = s_loc + bias.transpose(0, 2, 1, 3)[None].astype(jnp.float32)
        s_loc = s_loc.reshape(B, H, GRID_W, kh * kw)
        s_ctx = jnp.einsum('bchd,bnhd->bhcn', q_r, k_ctx, preferred_element_type=jnp.float32)
        p = jax.nn.softmax(jnp.concatenate([s_loc, s_ctx], axis=-1), axis=-1).astype(v.dtype)
        p_loc = p[..., :kh * kw].reshape(B, H, GRID_W, kh, kw)
        p_ctx = p[..., kh * kw:]
        return (jnp.einsum('bhcij,bicjhd->bchd', p_loc, v_win)
                + jnp.einsum('bhcn,bnhd->bchd', p_ctx, v_ctx))

    o = lax.map(row_block, (qg, r_idx, row_start))
    return o.transpose(1, 0, 2, 3, 4).reshape(B, L, H * Dh)


def context_attention(q, k, v):
    B, C, H, Dh = q.shape
    s = jnp.einsum('bqhd,bkhd->bhqk', q * (Dh ** -0.5), k, preferred_element_type=jnp.float32)
    p = jax.nn.softmax(s, axis=-1).astype(v.dtype)
    return jnp.einsum('bhqk,bkhd->bqhd', p, v).reshape(B, C, H * Dh)


def multiscale_pool(h, w, b, scale):
    B, L, D = h.shape
    hf = h.astype(jnp.float32)
    csum = jnp.concatenate([jnp.zeros((B, 1, D), jnp.float32), jnp.cumsum(hf, axis=1)], axis=1)
    t = jnp.arange(L)
    diffs = []
    for g, size in enumerate(POOL_SIZES):
        sl = slice(g * POOL_GROUP_DIM, (g + 1) * POOL_GROUP_DIM)
        lo = jnp.clip(t - size // 2, 0, L)
        hi = jnp.clip(t - size // 2 + size, 0, L)
        cnt = (hi - lo).astype(jnp.float32)[None, :, None]
        mean = (csum[:, hi, sl] - csum[:, lo, sl]) / cnt
        diffs.append(mean - hf[:, :, sl])
    diff = jnp.stack(diffs, axis=2).astype(h.dtype)
    y = jnp.einsum('blgc,gcd->blgd', diff, w) + b
    return y.reshape(B, L, D) * scale


def moe(x, w_router, b_router, w_up, b_up, w_down, b_down):
    B, L, D = x.shape
    xt = x.reshape(B * L, D)
    logits = (xt @ w_router + b_router).astype(jnp.float32)
    top_v, top_i = lax.top_k(logits, TOP_K)
    top_w = jax.nn.softmax(top_v, axis=-1)
    gates = jnp.sum(jax.nn.one_hot(top_i, N_EXPERTS, dtype=jnp.float32) * top_w[..., None], axis=1)
    out = jnp.zeros((B * L, D), jnp.float32)
    for e in range(N_EXPERTS):
        hid = xt @ w_up[e] + b_up[e]
        x_glu = jnp.minimum(hid[:, :D_EXPERT], SWIGLU_LIMIT)
        x_lin = jnp.clip(hid[:, D_EXPERT:], -SWIGLU_LIMIT, SWIGLU_LIMIT)
        act = x_glu * jax.nn.sigmoid(SWIGLU_ALPHA * x_glu) * (x_lin + 1)
        y = act @ w_down[e] + b_down[e]
        out = out + gates[:, e:e + 1] * y.astype(jnp.float32)
    return out.astype(x.dtype).reshape(B, L, D)


def setup_inputs(seed: int = 0) -> dict:
    key = jax.random.key(seed)
    ks = jax.random.split(key, 20)
    f32 = jnp.float32
    D = D_MODEL
    nrm = lambda k, shape, s: jax.random.normal(k, shape, f32) * s
    return {
        'x': nrm(ks[0], (BATCH, SEQ, D), 1.0),
        'c': nrm(ks[1], (BATCH, D), 1.0),
        'ctx': nrm(ks[2], (BATCH, CTX_LEN, D), 1.0),
        'c_ctx': nrm(ks[3], (D,), 1.0),
        'ada_w': nrm(ks[4], (DEPTH, D, 6 * D), 0.5 * D ** -0.5),
        'ada_b': nrm(ks[5], (DEPTH, 6 * D), 0.02),
        'norm_g': 1.0 + nrm(ks[6], (DEPTH, 4, D), 0.05),
        'na_w_qkv': nrm(ks[7], (N_ATTN_LAYERS, D, 3 * D), D ** -0.5),
        'na_w_o': nrm(ks[8], (N_ATTN_LAYERS, D, D), D ** -0.5),
        'na_rpb': nrm(ks[9], (N_ATTN_LAYERS, N_HEADS, 2 * NA_MAX_ROWS - 1, 2 * NA_COLS - 1), 0.1),
        'pool_w': nrm(ks[10], (N_POOL_LAYERS, N_POOL_GROUPS, POOL_GROUP_DIM, POOL_GROUP_DIM), POOL_GROUP_DIM ** -0.5),
        'pool_b': nrm(ks[11], (N_POOL_LAYERS, N_POOL_GROUPS, POOL_GROUP_DIM), 0.02),
        'pool_scale': 1.0 + nrm(ks[12], (N_POOL_LAYERS, D), 0.1),
        'moe_w_router': nrm(ks[13], (DEPTH, D, N_EXPERTS), D ** -0.5),
        'moe_b_router': nrm(ks[14], (DEPTH, N_EXPERTS), 0.01),
        'moe_w_up': nrm(ks[15], (DEPTH, N_EXPERTS, D, 2 * D_EXPERT), D ** -0.5),
        'moe_b_up': nrm(ks[16], (DEPTH, N_EXPERTS, 2 * D_EXPERT), 0.02),
        'moe_w_down': nrm(ks[17], (DEPTH, N_EXPERTS, D_EXPERT, D), D_EXPERT ** -0.5),
        'moe_b_down': nrm(ks[18], (DEPTH, N_EXPERTS, D), 0.02),
    }


def reference(x, c, ctx, c_ctx, ada_w, ada_b, norm_g, na_w_qkv, na_w_o, na_rpb, pool_w, pool_b, pool_scale,
              moe_w_router, moe_b_router, moe_w_up, moe_b_up, moe_w_down, moe_b_down):
    B, L, D = x.shape
    C = ctx.shape[1]
    h_ctx = ctx
    for i in range(DEPTH):
        mixer = i % N_MIXERS
        ctx_needed = any(j % N_MIXERS == 0 for j in range(i + 1, DEPTH))
        sh1, sc1, g1, sh2, sc2, g2 = [m[:, None, :] for m in adaln(c, ada_w[i], ada_b[i])]
        if mixer == 0 or ctx_needed:
            csh1, csc1, cg1, csh2, csc2, cg2 = adaln(c_ctx, ada_w[i], ada_b[i])
            hc = modulate(h_ctx, norm_g[i, 0], csh1, csc1)
        hx = modulate(x, norm_g[i, 0], sh1, sc1)
        if mixer == 0:
            a = i // N_MIXERS
            qkv_x = (hx @ na_w_qkv[a]).reshape(B, L, 3, N_HEADS, HEAD_DIM)
            qkv_c = (hc @ na_w_qkv[a]).reshape(B, C, 3, N_HEADS, HEAD_DIM)
            q, k, v = qkv_x[:, :, 0], qkv_x[:, :, 1], qkv_x[:, :, 2]
            qc, kc, vc = qkv_c[:, :, 0], qkv_c[:, :, 1], qkv_c[:, :, 2]
            mx = neighbourhood_attention(q, k, v, kc, vc, na_rpb[a]) @ na_w_o[a]
            if ctx_needed:
                mc = context_attention(qc, kc, vc) @ na_w_o[a]
        else:
            p = i // N_MIXERS
            mx = multiscale_pool(hx, pool_w[p], pool_b[p], pool_scale[p])
            if ctx_needed:
                mc = multiscale_pool(hc, pool_w[p], pool_b[p], pool_scale[p])
        x = x + g1 * rms_norm(mx, norm_g[i, 1])
        hx2 = modulate(x, norm_g[i, 2], sh2, sc2)
        if ctx_needed:
            h_ctx = h_ctx + cg1 * rms_norm(mc, norm_g[i, 1])
            hc2 = modulate(h_ctx, norm_g[i, 2], csh2, csc2)
            f_all = moe(jnp.concatenate([hx2, hc2], axis=1), moe_w_router[i], moe_b_router[i],
                        moe_w_up[i], moe_b_up[i], moe_w_down[i], moe_b_down[i])
            fx, fc = f_all[:, :L], f_all[:, L:]
            h_ctx = h_ctx + cg2 * rms_norm(fc, norm_g[i, 3])
        else:
            fx = moe(hx2, moe_w_router[i], moe_b_router[i], moe_w_up[i], moe_b_up[i],
                     moe_w_down[i], moe_b_down[i])
        x = x + g2 * rms_norm(fx, norm_g[i, 3])
    return x
```

```python
import functools

import numpy as np
import jax
import jax.numpy as jnp
from jax import lax
from jax.experimental import pallas as pl
from jax.experimental.pallas import tpu as pltpu

F32 = jnp.float32
BF16 = jnp.bfloat16
I32 = jnp.int32
U32 = jnp.uint32

D_MODEL = 1024
GRID_W = 64
N_HEADS = 16
HEAD_DIM = 64
NA_ROWS = 8
NA_COLS = 16
POOL_SIZES = (2, 4, 8, 16)
POOL_GROUP_DIM = D_MODEL // len(POOL_SIZES)
N_EXPERTS = 32
TOP_K = 4
D_EXPERT = D_MODEL
SWIGLU_LIMIT = 7.0
SWIGLU_ALPHA = 1.702
RMS_EPS = 1e-6

SUBLANES = 8
LANES = 128
VMEM_LIMIT_BYTES = 56 * 1024 * 1024

ATT_Q_ROWS = 4
ATT_K_ROWS = ATT_Q_ROWS + NA_ROWS - 1
ATT_QT = ATT_Q_ROWS * GRID_W
ATT_KT = ATT_K_ROWS * GRID_W
MASK_VALUE = -1e30

ROUTE_TM = 256
CHUNK = SUBLANES
ROUTE_SLOTS = ROUTE_TM * TOP_K + N_EXPERTS * CHUNK
EXPERT_TM = 256
HALF = D_MODEL // 2


def _params(semantics=None):
    return pltpu.CompilerParams(dimension_semantics=semantics, vmem_limit_bytes=VMEM_LIMIT_BYTES)


def _rms(x):
    return x * lax.rsqrt(jnp.mean(x * x, axis=-1, keepdims=True) + RMS_EPS)


def _pack_bf16_pair(lo, hi):
    lo_bits = lax.bitcast_convert_type(lo, U32) >> 16
    hi_bits = lax.bitcast_convert_type(hi, U32) & jnp.uint32(0xFFFF0000)
    return hi_bits | lo_bits


def _unpack_bf16_pair(u):
    lo = lax.bitcast_convert_type(u << 16, F32).astype(BF16)
    hi = lax.bitcast_convert_type(u & jnp.uint32(0xFFFF0000), F32).astype(BF16)
    return lo, hi


def _ada_kernel(c_ref, w_ref, b_ref, o_ref):
    cc = c_ref[...]
    s = cc * jax.nn.sigmoid(cc)
    o_ref[0] = jnp.dot(s, w_ref[0], precision=lax.Precision.HIGHEST,
                       preferred_element_type=F32) + b_ref[0]


def _ada_call(cc, ada_w, ada_b):
    depth, d, n = ada_w.shape
    nb = 1536
    return pl.pallas_call(
        _ada_kernel,
        grid=(depth, n // nb),
        in_specs=[
            pl.BlockSpec((SUBLANES, d), lambda l, j: (0, 0)),
            pl.BlockSpec((1, d, nb), lambda l, j: (l, 0, j)),
            pl.BlockSpec((1, 1, nb), lambda l, j: (l, 0, j)),
        ],
        out_specs=pl.BlockSpec((1, SUBLANES, nb), lambda l, j: (l, 0, j)),
        out_shape=jax.ShapeDtypeStruct((depth, SUBLANES, n), F32),
        compiler_params=_params(("arbitrary", "arbitrary")),
        name="adaln",
    )(cc, ada_w, ada_b.reshape(depth, 1, n))


def _qkv_kernel(x_ref, vec_ref, w_ref, o_ref):
    h = _rms(x_ref[...]) * vec_ref[0:1] * (1.0 + vec_ref[2:3]) + vec_ref[1:2]
    o_ref[...] = jnp.dot(h.astype(BF16), w_ref[...], preferred_element_type=F32).astype(BF16)


def _qkv_call(x, vec, w, tm):
    n, d = x.shape
    return pl.pallas_call(
        _qkv_kernel,
        grid=(n // tm,),
        in_specs=[
            pl.BlockSpec((tm, d), lambda i: (i, 0)),
            pl.BlockSpec((SUBLANES, d), lambda i: (0, 0)),
            pl.BlockSpec(w.shape, lambda i: (0, 0)),
        ],
        out_specs=pl.BlockSpec((tm, w.shape[1]), lambda i: (i, 0)),
        out_shape=jax.ShapeDtypeStruct((n, w.shape[1]), BF16),
        compiler_params=_params(("arbitrary",)),
        name="qkv",
    )(x, vec, w)


def _attn_bias_table(rpb, rows):
    n_groups = rows // ATT_Q_ROWS
    qi = np.arange(ATT_Q_ROWS)[:, None, None, None]
    c = np.arange(GRID_W)[None, :, None, None]
    j = np.arange(ATT_K_ROWS)[None, None, :, None]
    cj = np.arange(GRID_W)[None, None, None, :]
    col_start = np.clip(c - NA_COLS // 2, 0, GRID_W - NA_COLS)
    col_ok = (cj >= col_start) & (cj < col_start + NA_COLS)
    dc = np.clip(cj - c + NA_COLS - 1, 0, 2 * NA_COLS - 2)
    tabs = []
    for g in (0, 1, n_groups - 1):
        r0 = g * ATT_Q_ROWS
        kb = int(np.clip(r0 - NA_ROWS // 2, 0, rows - ATT_K_ROWS))
        r = r0 + qi
        row_start = np.clip(r - NA_ROWS // 2, 0, rows - NA_ROWS)
        key_row = kb + j
        row_ok = (key_row >= row_start) & (key_row < row_start + NA_ROWS)
        dr = np.clip(key_row - r + NA_ROWS - 1, 0, 2 * NA_ROWS - 2)
        shape = (ATT_Q_ROWS, GRID_W, ATT_K_ROWS, GRID_W)
        ok = np.broadcast_to(row_ok & col_ok, shape).reshape(ATT_QT, ATT_KT)
        dr_i = np.broadcast_to(dr, shape).reshape(ATT_QT, ATT_KT)
        dc_i = np.broadcast_to(dc, shape).reshape(ATT_QT, ATT_KT)
        vals = rpb[:, dr_i, dc_i]
        tabs.append(jnp.where(ok[None], vals, MASK_VALUE))
    return jnp.stack(tabs).astype(F32)


def _attn_kernel(rows, q_ref, k_ref, v_ref, kc_ref, vc_ref, bias_ref, o_ref):
    g = pl.program_id(1)
    kb = jnp.clip(g * ATT_Q_ROWS - NA_ROWS // 2, 0, rows - ATT_K_ROWS)
    start = pl.multiple_of(kb * GRID_W, GRID_W)
    q = q_ref[...]
    kl = k_ref[pl.ds(start, ATT_KT), :]
    vl = v_ref[pl.ds(start, ATT_KT), :]
    kc = kc_ref[...]
    vc = vc_ref[...]
    lane = lax.broadcasted_iota(I32, (ATT_QT, LANES), 1)
    nt = (((1,), (1,)), ((), ()))
    outs = []
    for h in range(2):
        mine = (lane < HEAD_DIM) if h == 0 else (lane >= HEAD_DIM)
        qm = jnp.where(mine, q, jnp.zeros_like(q))
        s = lax.dot_general(qm, kl, nt, preferred_element_type=F32) + bias_ref[0, h]
        sc = lax.dot_general(qm, kc, nt, preferred_element_type=F32)
        m = jnp.maximum(jnp.max(s, axis=-1, keepdims=True), jnp.max(sc, axis=-1, keepdims=True))
        p = jnp.exp(s - m)
        pc = jnp.exp(sc - m)
        den = jnp.sum(p, axis=-1, keepdims=True) + jnp.sum(pc, axis=-1, keepdims=True)
        o = (jnp.dot(p.astype(BF16), vl, preferred_element_type=F32)
             + jnp.dot(pc.astype(BF16), vc, preferred_element_type=F32))
        outs.append(o / den)
    o_ref[...] = jnp.where(lane < HEAD_DIM, outs[0], outs[1]).astype(BF16)


def _attn_call(qkv, qkv_ctx, bias_tab):
    n = qkv.shape[0]
    n_ctx = qkv_ctx.shape[0]
    rows = n // GRID_W
    n_groups = rows // ATT_Q_ROWS
    n_pairs = N_HEADS // 2

    def variant(g):
        return jnp.where(g == 0, 0, jnp.where(g == n_groups - 1, 2, 1))

    return pl.pallas_call(
        functools.partial(_attn_kernel, rows),
        grid=(n_pairs, n_groups),
        in_specs=[
            pl.BlockSpec((ATT_QT, LANES), lambda p, g: (g, p)),
            pl.BlockSpec((n, LANES), lambda p, g: (0, n_pairs + p)),
            pl.BlockSpec((n, LANES), lambda p, g: (0, 2 * n_pairs + p)),
            pl.BlockSpec((n_ctx, LANES), lambda p, g: (0, n_pairs + p)),
            pl.BlockSpec((n_ctx, LANES), lambda p, g: (0, 2 * n_pairs + p)),
            pl.BlockSpec((1, 2, ATT_QT, ATT_KT), lambda p, g: (variant(g), p, 0, 0)),
        ],
        out_specs=pl.BlockSpec((ATT_QT, LANES), lambda p, g: (g, p)),
        out_shape=jax.ShapeDtypeStruct((n, D_MODEL), BF16),
        compiler_params=_params(("arbitrary", "arbitrary")),
        name="natten",
    )(qkv, qkv, qkv, qkv_ctx, qkv_ctx, bias_tab)


POOL_HALO = 8


def _pool_kernel(n_tokens, tm, xp_ref, xc_ref, xn_ref, vec_ref, w_ref, b_ref, o_ref):
    i = pl.program_id(0)
    xcat = jnp.concatenate([xp_ref[...], xc_ref[...], xn_ref[...]], axis=0)
    h = _rms(xcat) * vec_ref[0:1] * (1.0 + vec_ref[2:3]) + vec_ref[1:2]
    row = lax.broadcasted_iota(I32, (tm + 2 * POOL_HALO, 1), 0) + i * tm - POOL_HALO
    h = jnp.where((row >= 0) & (row < n_tokens), h, 0.0)
    t = lax.broadcasted_iota(I32, (tm, 1), 0) + i * tm
    outs = []
    for g, size in enumerate(POOL_SIZES):
        sl = slice(g * POOL_GROUP_DIM, (g + 1) * POOL_GROUP_DIM)
        hg = h[:, sl]
        acc = hg[POOL_HALO - size // 2:POOL_HALO - size // 2 + tm]
        for j in range(-size // 2 + 1, size // 2):
            acc = acc + hg[POOL_HALO + j:POOL_HALO + j + tm]
        lo = jnp.clip(t - size // 2, 0, n_tokens)
        hi = jnp.clip(t - size // 2 + size, 0, n_tokens)
        diff = acc / (hi - lo).astype(F32) - hg[POOL_HALO:POOL_HALO + tm]
        y = jnp.dot(diff.astype(BF16), w_ref[g], preferred_element_type=F32) + b_ref[g:g + 1]
        outs.append(y)
    o_ref[...] = jnp.concatenate(outs, axis=1) * vec_ref[3:4]


def _pool_call(x, vec, w, b, tm):
    n, d = x.shape
    per = tm // POOL_HALO
    last = n // POOL_HALO - 1
    return pl.pallas_call(
        functools.partial(_pool_kernel, n, tm),
        grid=(n // tm,),
        in_specs=[
            pl.BlockSpec((POOL_HALO, d), lambda i: (jnp.maximum(i * per - 1, 0), 0)),
            pl.BlockSpec((tm, d), lambda i: (i, 0)),
            pl.BlockSpec((POOL_HALO, d), lambda i: (jnp.minimum((i + 1) * per, last), 0)),
            pl.BlockSpec((SUBLANES, d), lambda i: (0, 0)),
            pl.BlockSpec(w.shape, lambda i: (0, 0, 0)),
            pl.BlockSpec(b.shape, lambda i: (0, 0)),
        ],
        out_specs=pl.BlockSpec((tm, d), lambda i: (i, 0)),
        out_shape=jax.ShapeDtypeStruct((n, d), F32),
        compiler_params=_params(("arbitrary",)),
        name="pool_mixer",
    )(x, x, x, vec, w, b)


def _route_kernel(with_proj, x_ref, mix_ref, wo_ref, vec_ref, wrt_ref, br_ref,
                  xn_ref, z_ref, slot_ref, gate_ref, meta_ref, carry_ref):
    i = pl.program_id(0)

    @pl.when(i == 0)
    def _():
        carry_ref[...] = jnp.zeros_like(carry_ref)

    if with_proj:
        mx = jnp.dot(mix_ref[...], wo_ref[...], preferred_element_type=F32)
    else:
        mx = mix_ref[...]
    xn = x_ref[...] + vec_ref[1:2] * (_rms(mx) * vec_ref[0:1])
    xn_ref[...] = xn
    h2 = _rms(xn) * vec_ref[2:3] * (1.0 + vec_ref[4:5]) + vec_ref[3:4]

    logits = lax.dot_general(wrt_ref[...], h2, (((1,), (1,)), ((), ())),
                             precision=lax.Precision.HIGHEST,
                             preferred_element_type=F32) + br_ref[:, 0:1]
    ie = lax.broadcasted_iota(I32, (N_EXPERTS, ROUTE_TM), 0)
    vals, hots = [], []
    for _ in range(TOP_K):
        m = jnp.max(logits, axis=0, keepdims=True)
        idx = jnp.min(jnp.where(logits == m, ie, N_EXPERTS), axis=0, keepdims=True)
        hot = ie == idx
        vals.append(m)
        hots.append(hot)
        logits = jnp.where(hot, -jnp.inf, logits)
    exps = [jnp.exp(v - vals[0]) for v in vals]
    den = exps[0] + exps[1] + exps[2] + exps[3]

    assign = (hots[0] | hots[1] | hots[2] | hots[3]).astype(F32)
    tr = lax.broadcasted_iota(I32, (ROUTE_TM, ROUTE_TM), 0)
    tc = lax.broadcasted_iota(I32, (ROUTE_TM, ROUTE_TM), 1)
    before = (tr < tc).astype(BF16)
    prefix = jnp.dot(assign.astype(BF16), before, preferred_element_type=F32)
    count = jnp.sum(assign, axis=1, keepdims=True)
    count8 = jnp.ceil(count * (1.0 / CHUNK)) * CHUNK
    count8_b = jnp.broadcast_to(count8, (N_EXPERTS, LANES))
    er = lax.broadcasted_iota(I32, (N_EXPERTS, LANES), 0)
    ec = lax.broadcasted_iota(I32, (N_EXPERTS, LANES), 1)
    count8_pad = jnp.concatenate([count8_b, jnp.zeros((LANES - N_EXPERTS, LANES), F32)], axis=0)
    seg_start = jnp.dot((ec < er).astype(BF16), count8_pad.astype(BF16), preferred_element_type=F32)
    base = seg_start[:, 0:1] + prefix
    slots = [jnp.sum(jnp.where(hot, base, 0.0), axis=0, keepdims=True).astype(I32) for hot in hots]

    sj = lax.broadcasted_iota(I32, (ROUTE_SLOTS, ROUTE_TM), 0)
    place = (sj == slots[0]) | (sj == slots[1]) | (sj == slots[2]) | (sj == slots[3])
    z = jnp.dot(place.astype(BF16), h2.astype(BF16), preferred_element_type=F32)
    z_ref[...] = _pack_bf16_pair(z[:, :HALF], z[:, HALF:])

    zero_i = jnp.zeros((SUBLANES - TOP_K, ROUTE_TM), I32)
    slot_ref[...] = jnp.concatenate(slots + [zero_i], axis=0)
    gate_ref[...] = jnp.concatenate([e / den for e in exps] + [zero_i.astype(F32)], axis=0)

    carry = carry_ref[...]
    lane = lax.broadcasted_iota(I32, (N_EXPERTS, LANES), 1)
    meta_ref[0] = jnp.where(lane == 0, count8_b, jnp.where(lane == 1, seg_start, carry))
    carry_ref[...] = carry + count8_b


def _route_call(with_proj, x, mix, wo, vec, wrt, br):
    n, d = x.shape
    nb = n // ROUTE_TM
    blk = lambda i: (i, 0)
    fix = lambda i: (0, 0)
    return pl.pallas_call(
        functools.partial(_route_kernel, with_proj),
        grid=(nb,),
        in_specs=[
            pl.BlockSpec((ROUTE_TM, d), blk),
            pl.BlockSpec((ROUTE_TM, d), blk),
            pl.BlockSpec(wo.shape, fix),
            pl.BlockSpec((SUBLANES, d), fix),
            pl.BlockSpec(wrt.shape, fix),
            pl.BlockSpec(br.shape, fix),
        ],
        out_specs=[
            pl.BlockSpec((ROUTE_TM, d), blk),
            pl.BlockSpec((ROUTE_SLOTS, HALF), blk),
            pl.BlockSpec((SUBLANES, ROUTE_TM), lambda i: (0, i)),
            pl.BlockSpec((SUBLANES, ROUTE_TM), lambda i: (0, i)),
            pl.BlockSpec((1, N_EXPERTS, LANES), lambda i: (i, 0, 0)),
        ],
        out_shape=[
            jax.ShapeDtypeStruct((n, d), F32),
            jax.ShapeDtypeStruct((nb * ROUTE_SLOTS, HALF), U32),
            jax.ShapeDtypeStruct((SUBLANES, n), I32),
            jax.ShapeDtypeStruct((SUBLANES, n), F32),
            jax.ShapeDtypeStruct((nb, N_EXPERTS, LANES), F32),
        ],
        scratch_shapes=[pltpu.VMEM((N_EXPERTS, LANES), F32)],
        compiler_params=_params(("arbitrary",)),
        name="route_sort",
    )(x, mix, wo, vec, wrt, br)


def _chunk_copy(src_hbm, dst_hbm, sem, a, b):
    src = src_hbm if a is None else src_hbm.at[pl.ds(pl.multiple_of(a, CHUNK), CHUNK)]
    return pltpu.make_async_copy(src, dst_hbm.at[pl.ds(pl.multiple_of(b, CHUNK), CHUNK)], sem)


def _move_kernel(group, n_groups, src_ref, dst_ref, nch_ref, fdst_ref, fnch_ref,
                 in_hbm, zero_hbm, out_hbm, sem, fsem):
    def for_chunks(n, fn):
        lax.fori_loop(0, n, lambda c, _: (fn(c * CHUNK), 0)[1], 0)

    def per_group(gi, _):
        def seg_start(s, _):
            a, b = src_ref[s], dst_ref[s]
            for_chunks(nch_ref[s], lambda o: _chunk_copy(in_hbm, out_hbm, sem, a + o, b + o).start())
            return 0

        def seg_wait(s, _):
            a, b = src_ref[s], dst_ref[s]
            for_chunks(nch_ref[s], lambda o: _chunk_copy(in_hbm, out_hbm, sem, a + o, b + o).wait())
            return 0

        lax.fori_loop(gi * group, (gi + 1) * group, seg_start, 0)
        lax.fori_loop(gi * group, (gi + 1) * group, seg_wait, 0)
        return 0

    lax.fori_loop(0, n_groups, per_group, 0)

    n_fill = fdst_ref.shape[0]

    def fill_start(f, _):
        b = fdst_ref[f]
        for_chunks(fnch_ref[f], lambda o: _chunk_copy(zero_hbm, out_hbm, fsem, None, b + o).start())
        return 0

    def fill_wait(f, _):
        b = fdst_ref[f]
        for_chunks(fnch_ref[f], lambda o: _chunk_copy(zero_hbm, out_hbm, fsem, None, b + o).wait())
        return 0

    lax.fori_loop(0, n_fill, fill_start, 0)
    lax.fori_loop(0, n_fill, fill_wait, 0)


def _move_call(src, dst, nch, fdst, fnch, data, out_rows, group):
    n_seg = src.shape[0]
    zero = jnp.zeros((CHUNK, data.shape[1]), data.dtype)
    smem = pl.BlockSpec(memory_space=pltpu.SMEM)
    hbm = pl.BlockSpec(memory_space=pl.ANY)
    return pl.pallas_call(
        functools.partial(_move_kernel, group, n_seg // group),
        in_specs=[smem, smem, smem, smem, smem, hbm, hbm],
        out_specs=hbm,
        out_shape=jax.ShapeDtypeStruct((out_rows, data.shape[1]), data.dtype),
        scratch_shapes=[pltpu.SemaphoreType.DMA(()), pltpu.SemaphoreType.DMA(())],
        name="chunk_move",
    )(src, dst, nch, fdst, fnch, data, zero)


def _expert_kernel(te_ref, nt_ref, xs_ref, wu_ref, bu_ref, wd_ref, bd_ref, ys_ref, wub_ref, wdb_ref):
    i = pl.program_id(0)
    valid = i < nt_ref[0]
    changed = (i == 0) | (te_ref[i] != te_ref[jnp.maximum(i - 1, 0)])

    @pl.when(valid & changed)
    def _():
        rows = 128

        def cast(r, _):
            sl = pl.ds(pl.multiple_of(r * rows, rows), rows)
            wub_ref[sl, :] = wu_ref[0, sl, :].astype(BF16)
            wdb_ref[sl, :] = wd_ref[0, sl, :].astype(BF16)
            return 0

        lax.fori_loop(0, D_MODEL // rows, cast, 0)

    @pl.when(valid)
    def _():
        lo, hi = _unpack_bf16_pair(xs_ref[...])
        h = (jnp.dot(lo, wub_ref[0:HALF, :], preferred_element_type=F32)
             + jnp.dot(hi, wub_ref[HALF:, :], preferred_element_type=F32) + bu_ref[0])
        glu = jnp.minimum(h[:, :D_EXPERT], SWIGLU_LIMIT)
        lin = jnp.clip(h[:, D_EXPERT:], -SWIGLU_LIMIT, SWIGLU_LIMIT)
        act = glu * jax.nn.sigmoid(SWIGLU_ALPHA * glu) * (lin + 1.0)
        y = jnp.dot(act.astype(BF16), wdb_ref[...], preferred_element_type=F32) + bd_ref[0]
        yb = y.astype(BF16).astype(F32)
        ys_ref[...] = _pack_bf16_pair(yb[:, :HALF], yb[:, HALF:])

    @pl.when(jnp.logical_not(valid))
    def _():
        ys_ref[...] = jnp.zeros_like(ys_ref)


def _expert_call(tile_expert, n_tiles, xs, w_up, b_up, w_down, b_down):
    max_tiles = tile_expert.shape[0]
    e, d, n_up = w_up.shape

    def row_map(i, te, nt):
        return (jnp.minimum(i, jnp.maximum(nt[0] - 1, 0)), 0)

    def w_map(i, te, nt):
        return (te[i], 0, 0)

    return pl.pallas_call(
        _expert_kernel,
        grid_spec=pltpu.PrefetchScalarGridSpec(
            num_scalar_prefetch=2,
            grid=(max_tiles,),
            in_specs=[
                pl.BlockSpec((EXPERT_TM, HALF), row_map),
                pl.BlockSpec((1, d, n_up), w_map),
                pl.BlockSpec((1, 1, n_up), w_map),
                pl.BlockSpec((1, D_EXPERT, d), w_map),
                pl.BlockSpec((1, 1, d), w_map),
            ],
            out_specs=pl.BlockSpec((EXPERT_TM, HALF), lambda i, te, nt: (i, 0)),
            scratch_shapes=[pltpu.VMEM((d, n_up), BF16), pltpu.VMEM((D_EXPERT, d), BF16)],
        ),
        out_shape=jax.ShapeDtypeStruct((max_tiles * EXPERT_TM, HALF), U32),
        compiler_params=_params(("arbitrary",)),
        name="expert_mlp",
    )(tile_expert, n_tiles, xs, w_up, b_up.reshape(e, 1, n_up), w_down, b_down.reshape(e, 1, d))


def _combine_kernel(zy_ref, slot_ref, gate_ref, xn_ref, vec_ref, o_ref):
    lo, hi = _unpack_bf16_pair(zy_ref[...])
    sj = lax.broadcasted_iota(I32, (ROUTE_TM, ROUTE_SLOTS), 1)
    slot = slot_ref[...]
    gate = gate_ref[...]
    g = jnp.zeros((ROUTE_TM, ROUTE_SLOTS), F32)
    for k in range(TOP_K):
        g = g + jnp.where(sj == slot[:, k:k + 1], gate[:, k:k + 1], 0.0)
    gb = g.astype(BF16)
    f = jnp.concatenate([jnp.dot(gb, lo, preferred_element_type=F32),
                         jnp.dot(gb, hi, preferred_element_type=F32)], axis=1)
    o_ref[...] = xn_ref[...] + vec_ref[1:2] * (_rms(f) * vec_ref[0:1])


def _combine_call(zy, slot_t, gate_t, xn, vec):
    n, d = xn.shape
    blk = lambda i: (i, 0)
    return pl.pallas_call(
        _combine_kernel,
        grid=(n // ROUTE_TM,),
        in_specs=[
            pl.BlockSpec((ROUTE_SLOTS, HALF), blk),
            pl.BlockSpec((ROUTE_TM, SUBLANES), blk),
            pl.BlockSpec((ROUTE_TM, SUBLANES), blk),
            pl.BlockSpec((ROUTE_TM, d), blk),
            pl.BlockSpec((SUBLANES, d), lambda i: (0, 0)),
        ],
        out_specs=pl.BlockSpec((ROUTE_TM, d), blk),
        out_shape=jax.ShapeDtypeStruct((n, d), F32),
        compiler_params=_params(("arbitrary",)),
        name="combine",
    )(zy, slot_t, gate_t, xn, vec)


def _vec(*rows):
    v = jnp.stack([r.astype(F32) for r in rows])
    return jnp.pad(v, ((0, SUBLANES - v.shape[0]), (0, 0)))


def _moe_layer(x, mix, wo, with_proj, route_vec, combine_vec, w_router, b_router, w_up, b_up, w_down, b_down):
    n = x.shape[0]
    nb = n // ROUTE_TM
    br = jnp.broadcast_to(b_router.astype(F32)[:, None], (N_EXPERTS, LANES))
    xn, z, slots, gates, meta = _route_call(with_proj, x, mix, wo, route_vec, w_router.T, br)

    count8 = meta[:, :, 0].astype(I32)
    seg_start = meta[:, :, 1].astype(I32)
    carry = meta[:, :, 2].astype(I32)
    total = jnp.sum(count8, axis=0)
    region = (total + EXPERT_TM - 1) // EXPERT_TM * EXPERT_TM
    region_end = jnp.cumsum(region)
    offs = region_end - region
    z_row = (jnp.arange(nb, dtype=I32)[:, None] * ROUTE_SLOTS + seg_start).reshape(-1)
    e_row = (offs[None, :] + carry).reshape(-1)
    nch = (count8 // CHUNK).reshape(-1)
    max_rows = n * TOP_K + nb * N_EXPERTS * (CHUNK - 1) + N_EXPERTS * (EXPERT_TM - CHUNK)
    max_tiles = -(-max_rows // EXPERT_TM)
    n_tiles = (region_end[-1] // EXPERT_TM).astype(I32)
    tile_ids = jnp.arange(max_tiles, dtype=I32)
    tile_end = region_end // EXPERT_TM
    te = jnp.searchsorted(tile_end, jnp.minimum(tile_ids, jnp.maximum(n_tiles - 1, 0)), side="right")
    te = jnp.minimum(te, N_EXPERTS - 1).astype(I32)

    used = jnp.sum(count8, axis=1)
    fill_row = jnp.concatenate([offs + total, region_end[-1:]])
    fill_nch = jnp.concatenate([region - total, max_tiles * EXPERT_TM - region_end[-1:]]) // CHUNK
    xs = _move_call(z_row, e_row, nch, fill_row, fill_nch, z, max_tiles * EXPERT_TM, N_EXPERTS)
    ys = _expert_call(te, n_tiles.reshape(1), xs, w_up, b_up, w_down, b_down)
    zy = _move_call(e_row, z_row, nch, jnp.arange(nb, dtype=I32) * ROUTE_SLOTS + used,
                    (ROUTE_SLOTS - used) // CHUNK, ys, nb * ROUTE_SLOTS, N_EXPERTS)
    return _combine_call(zy, slots.T, gates.T, xn, combine_vec)


def kernel(x, c, ctx, c_ctx, ada_w, ada_b, norm_g, na_w_qkv, na_w_o, na_rpb, pool_w, pool_b, pool_scale,
           moe_w_router, moe_b_router, moe_w_up, moe_b_up, moe_w_down, moe_b_down):
    b, n, d = x.shape
    assert b == 1 and d == D_MODEL and ada_w.shape[0] == 2 and n % (GRID_W * ATT_Q_ROWS) == 0
    x2 = x[0]
    ctx2 = ctx[0]
    rows = n // GRID_W

    cc = jnp.zeros((SUBLANES, d), F32).at[0].set(c[0]).at[1].set(c_ctx)
    mod = _ada_call(cc, ada_w, ada_b)

    def six(layer, who):
        return [mod[layer, who, j * d:(j + 1) * d] for j in range(6)]

    sh1, sc1, g1, sh2, sc2, g2 = six(0, 0)
    csh1, csc1 = six(0, 1)[:2]
    q_scale = jnp.concatenate([jnp.full((d,), HEAD_DIM ** -0.5, F32), jnp.ones((2 * d,), F32)])
    wqkv = (na_w_qkv[0] * q_scale).astype(BF16)
    qkv = _qkv_call(x2, _vec(norm_g[0, 0], sh1, sc1), wqkv, 512)
    qkv_ctx = _qkv_call(ctx2, _vec(norm_g[0, 0], csh1, csc1), wqkv, ctx2.shape[0])
    att = _attn_call(qkv, qkv_ctx, _attn_bias_table(na_rpb[0], rows))
    x2 = _moe_layer(x2, att, na_w_o[0].astype(BF16), True,
                    _vec(norm_g[0, 1], g1, norm_g[0, 2], sh2, sc2), _vec(norm_g[0, 3], g2),
                    moe_w_router[0], moe_b_router[0], moe_w_up[0], moe_b_up[0], moe_w_down[0], moe_b_down[0])

    sh1, sc1, g1, sh2, sc2, g2 = six(1, 0)
    mix = _pool_call(x2, _vec(norm_g[1, 0], sh1, sc1, pool_scale[0]), pool_w[0].astype(BF16),
                     jnp.pad(pool_b[0], ((0, SUBLANES - len(POOL_SIZES)), (0, 0))), 512)
    x2 = _moe_layer(x2, mix, jnp.zeros((SUBLANES, LANES), BF16), False,
                    _vec(norm_g[1, 1], g1, norm_g[1, 2], sh2, sc2), _vec(norm_g[1, 3], g2),
                    moe_w_router[1], moe_b_router[1], moe_w_up[1], moe_b_up[1], moe_w_down[1], moe_b_down[1])
    return x2[None]
```

```python
import functools

import numpy as np
import jax
import jax.numpy as jnp
from jax import lax
from jax.experimental import pallas as pl
from jax.experimental.pallas import tpu as pltpu

F32 = jnp.float32
BF16 = jnp.bfloat16
I32 = jnp.int32
U32 = jnp.uint32

D_MODEL = 1024
GRID_W = 64
N_HEADS = 16
HEAD_DIM = 64
NA_ROWS = 8
NA_COLS = 16
POOL_SIZES = (2, 4, 8, 16)
POOL_GROUP_DIM = D_MODEL // len(POOL_SIZES)
N_EXPERTS = 32
TOP_K = 4
D_EXPERT = D_MODEL
SWIGLU_LIMIT = 7.0
SWIGLU_ALPHA = 1.702
RMS_EPS = 1e-6

SUBLANES = 8
LANES = 128
VMEM_LIMIT_BYTES = 56 * 1024 * 1024

ATT_Q_ROWS = 4
ATT_K_ROWS = ATT_Q_ROWS + NA_ROWS - 1
ATT_QT = ATT_Q_ROWS * GRID_W
ATT_KT = ATT_K_ROWS * GRID_W
MASK_VALUE = -1e30

ROUTE_TM = 256
CHUNK = SUBLANES
ROUTE_SLOTS = ROUTE_TM * TOP_K + N_EXPERTS * CHUNK
EXPERT_TM = 256
HALF = D_MODEL // 2


def _params(semantics=None):
    return pltpu.CompilerParams(dimension_semantics=semantics, vmem_limit_bytes=VMEM_LIMIT_BYTES)


def _rms(x):
    return x * lax.rsqrt(jnp.mean(x * x, axis=-1, keepdims=True) + RMS_EPS)


def _pack_bf16_pair(lo, hi):
    lo_bits = lax.bitcast_convert_type(lo, U32) >> 16
    hi_bits = lax.bitcast_convert_type(hi, U32) & jnp.uint32(0xFFFF0000)
    return hi_bits | lo_bits


def _unpack_bf16_pair(u):
    lo = lax.bitcast_convert_type(u << 16, F32).astype(BF16)
    hi = lax.bitcast_convert_type(u & jnp.uint32(0xFFFF0000), F32).astype(BF16)
    return lo, hi


def _ada_kernel(c_ref, w_ref, b_ref, o_ref):
    cc = c_ref[...]
    s = cc * jax.nn.sigmoid(cc)
    o_ref[0] = jnp.dot(s, w_ref[0], precision=lax.Precision.HIGHEST,
                       preferred_element_type=F32) + b_ref[0]


def _ada_call(cc, ada_w, ada_b):
    depth, d, n = ada_w.shape
    nb = 1536
    return pl.pallas_call(
        _ada_kernel,
        grid=(depth, n // nb),
        in_specs=[
            pl.BlockSpec((SUBLANES, d), lambda l, j: (0, 0)),
            pl.BlockSpec((1, d, nb), lambda l, j: (l, 0, j)),
            pl.BlockSpec((1, 1, nb), lambda l, j: (l, 0, j)),
        ],
        out_specs=pl.BlockSpec((1, SUBLANES, nb), lambda l, j: (l, 0, j)),
        out_shape=jax.ShapeDtypeStruct((depth, SUBLANES, n), F32),
        compiler_params=_params(("arbitrary", "arbitrary")),
        name="adaln",
    )(cc, ada_w, ada_b.reshape(depth, 1, n))


def _qkv_kernel(x_ref, vec_ref, w_ref, o_ref):
    h = _rms(x_ref[...]) * vec_ref[0:1] * (1.0 + vec_ref[2:3]) + vec_ref[1:2]
    o_ref[...] = jnp.dot(h.astype(BF16), w_ref[...], preferred_element_type=F32).astype(BF16)


def _qkv_call(x, vec, w, tm):
    n, d = x.shape
    return pl.pallas_call(
        _qkv_kernel,
        grid=(n // tm,),
        in_specs=[
            pl.BlockSpec((tm, d), lambda i: (i, 0)),
            pl.BlockSpec((SUBLANES, d), lambda i: (0, 0)),
            pl.BlockSpec(w.shape, lambda i: (0, 0)),
        ],
        out_specs=pl.BlockSpec((tm, w.shape[1]), lambda i: (i, 0)),
        out_shape=jax.ShapeDtypeStruct((n, w.shape[1]), BF16),
        compiler_params=_params(("arbitrary",)),
        name="qkv",
    )(x, vec, w)


def _attn_bias_table(rpb, rows):
    n_groups = rows // ATT_Q_ROWS
    n_heads, _, n_dc = rpb.shape
    period = 2 * GRID_W
    c = np.arange(GRID_W)[:, None]
    cj = np.arange(GRID_W)[None, :]
    col_start = np.clip(c - NA_COLS // 2, 0, GRID_W - NA_COLS)
    col_ok = (cj >= col_start) & (cj < col_start + NA_COLS)
    lead = GRID_W - 1 - (NA_COLS - 1)
    masked_row = jnp.full((n_heads, n_dc), MASK_VALUE, F32)
    tabs = []
    for g in (0, 1, n_groups - 1):
        r0 = g * ATT_Q_ROWS
        kb = int(np.clip(r0 - NA_ROWS // 2, 0, rows - ATT_K_ROWS))
        per_q = []
        for qi in range(ATT_Q_ROWS):
            r = r0 + qi
            row_start = int(np.clip(r - NA_ROWS // 2, 0, rows - NA_ROWS))
            per_k = []
            for j in range(ATT_K_ROWS):
                key_row = kb + j
                inside = row_start <= key_row < row_start + NA_ROWS
                per_k.append(rpb[:, key_row - r + NA_ROWS - 1, :].astype(F32) if inside else masked_row)
            per_q.append(jnp.stack(per_k, axis=1))
        rowpart = jnp.stack(per_q, axis=1)
        padded = jnp.pad(rowpart, ((0, 0), (0, 0), (0, 0), (lead, period - lead - n_dc)),
                         constant_values=MASK_VALUE)
        tiled = jnp.tile(padded, (1, 1, 1, GRID_W))[..., :GRID_W * (period - 1)]
        skew = tiled.reshape(n_heads, ATT_Q_ROWS, ATT_K_ROWS, GRID_W, period - 1)[..., GRID_W - 1:]
        full = jnp.where(col_ok, skew, MASK_VALUE)
        tabs.append(full.transpose(0, 1, 3, 2, 4).reshape(n_heads, ATT_QT, ATT_KT))
    return jnp.stack(tabs)


def _attn_kernel(rows, q_ref, k_ref, v_ref, kc_ref, vc_ref, bias_ref, o_ref):
    g = pl.program_id(1)
    kb = jnp.clip(g * ATT_Q_ROWS - NA_ROWS // 2, 0, rows - ATT_K_ROWS)
    start = pl.multiple_of(kb * GRID_W, GRID_W)
    q = q_ref[...]
    kl = k_ref[pl.ds(start, ATT_KT), :]
    vl = v_ref[pl.ds(start, ATT_KT), :]
    kc = kc_ref[...]
    vc = vc_ref[...]
    lane = lax.broadcasted_iota(I32, (ATT_QT, LANES), 1)
    nt = (((1,), (1,)), ((), ()))
    outs = []
    for h in range(2):
        mine = (lane < HEAD_DIM) if h == 0 else (lane >= HEAD_DIM)
        qm = jnp.where(mine, q, jnp.zeros_like(q))
        s = lax.dot_general(qm, kl, nt, preferred_element_type=F32) + bias_ref[0, h]
        sc = lax.dot_general(qm, kc, nt, preferred_element_type=F32)
        m = jnp.maximum(jnp.max(s, axis=-1, keepdims=True), jnp.max(sc, axis=-1, keepdims=True))
        p = jnp.exp(s - m)
        pc = jnp.exp(sc - m)
        den = jnp.sum(p, axis=-1, keepdims=True) + jnp.sum(pc, axis=-1, keepdims=True)
        o = (jnp.dot(p.astype(BF16), vl, preferred_element_type=F32)
             + jnp.dot(pc.astype(BF16), vc, preferred_element_type=F32))
        outs.append(o / den)
    o_ref[...] = jnp.where(lane < HEAD_DIM, outs[0], outs[1]).astype(BF16)


def _attn_call(qkv, qkv_ctx, bias_tab):
    n = qkv.shape[0]
    n_ctx = qkv_ctx.shape[0]
    rows = n // GRID_W
    n_groups = rows // ATT_Q_ROWS
    n_pairs = N_HEADS // 2

    def variant(g):
        return jnp.where(g == 0, 0, jnp.where(g == n_groups - 1, 2, 1))

    return pl.pallas_call(
        functools.partial(_attn_kernel, rows),
        grid=(n_pairs, n_groups),
        in_specs=[
            pl.BlockSpec((ATT_QT, LANES), lambda p, g: (g, p)),
            pl.BlockSpec((n, LANES), lambda p, g: (0, n_pairs + p)),
            pl.BlockSpec((n, LANES), lambda p, g: (0, 2 * n_pairs + p)),
            pl.BlockSpec((n_ctx, LANES), lambda p, g: (0, n_pairs + p)),
            pl.BlockSpec((n_ctx, LANES), lambda p, g: (0, 2 * n_pairs + p)),
            pl.BlockSpec((1, 2, ATT_QT, ATT_KT), lambda p, g: (variant(g), p, 0, 0)),
        ],
        out_specs=pl.BlockSpec((ATT_QT, LANES), lambda p, g: (g, p)),
        out_shape=jax.ShapeDtypeStruct((n, D_MODEL), BF16),
        compiler_params=_params(("arbitrary", "arbitrary")),
        name="natten",
    )(qkv, qkv, qkv, qkv_ctx, qkv_ctx, bias_tab)


POOL_HALO = 8


def _pool_kernel(n_tokens, tm, xp_ref, xc_ref, xn_ref, vec_ref, w_ref, b_ref, o_ref):
    i = pl.program_id(0)
    xcat = jnp.concatenate([xp_ref[...], xc_ref[...], xn_ref[...]], axis=0)
    h = _rms(xcat) * vec_ref[0:1] * (1.0 + vec_ref[2:3]) + vec_ref[1:2]
    row = lax.broadcasted_iota(I32, (tm + 2 * POOL_HALO, 1), 0) + i * tm - POOL_HALO
    h = jnp.where((row >= 0) & (row < n_tokens), h, 0.0)
    t = lax.broadcasted_iota(I32, (tm, 1), 0) + i * tm
    outs = []
    for g, size in enumerate(POOL_SIZES):
        sl = slice(g * POOL_GROUP_DIM, (g + 1) * POOL_GROUP_DIM)
        hg = h[:, sl]
        acc = hg[POOL_HALO - size // 2:POOL_HALO - size // 2 + tm]
        for j in range(-size // 2 + 1, size // 2):
            acc = acc + hg[POOL_HALO + j:POOL_HALO + j + tm]
        lo = jnp.clip(t - size // 2, 0, n_tokens)
        hi = jnp.clip(t - size // 2 + size, 0, n_tokens)
        diff = acc / (hi - lo).astype(F32) - hg[POOL_HALO:POOL_HALO + tm]
        y = jnp.dot(diff.astype(BF16), w_ref[g], preferred_element_type=F32) + b_ref[g:g + 1]
        outs.append(y)
    o_ref[...] = jnp.concatenate(outs, axis=1) * vec_ref[3:4]


def _pool_call(x, vec, w, b, tm):
    n, d = x.shape
    per = tm // POOL_HALO
    last = n // POOL_HALO - 1
    return pl.pallas_call(
        functools.partial(_pool_kernel, n, tm),
        grid=(n // tm,),
        in_specs=[
            pl.BlockSpec((POOL_HALO, d), lambda i: (jnp.maximum(i * per - 1, 0), 0)),
            pl.BlockSpec((tm, d), lambda i: (i, 0)),
            pl.BlockSpec((POOL_HALO, d), lambda i: (jnp.minimum((i + 1) * per, last), 0)),
            pl.BlockSpec((SUBLANES, d), lambda i: (0, 0)),
            pl.BlockSpec(w.shape, lambda i: (0, 0, 0)),
            pl.BlockSpec(b.shape, lambda i: (0, 0)),
        ],
        out_specs=pl.BlockSpec((tm, d), lambda i: (i, 0)),
        out_shape=jax.ShapeDtypeStruct((n, d), F32),
        compiler_params=_params(("arbitrary",)),
        name="pool_mixer",
    )(x, x, x, vec, w, b)


def _route_kernel(with_proj, x_ref, mix_ref, wo_ref, vec_ref, wrt_ref, br_ref,
                  xn_ref, z_ref, slot_ref, gate_ref, meta_ref, carry_ref):
    i = pl.program_id(0)

    @pl.when(i == 0)
    def _():
        carry_ref[...] = jnp.zeros_like(carry_ref)

    if with_proj:
        mx = jnp.dot(mix_ref[...], wo_ref[...], preferred_element_type=F32)
    else:
        mx = mix_ref[...]
    xn = x_ref[...] + vec_ref[1:2] * (_rms(mx) * vec_ref[0:1])
    xn_ref[...] = xn
    h2 = _rms(xn) * vec_ref[2:3] * (1.0 + vec_ref[4:5]) + vec_ref[3:4]

    logits = lax.dot_general(wrt_ref[...], h2, (((1,), (1,)), ((), ())),
                             precision=lax.Precision.HIGHEST,
                             preferred_element_type=F32) + br_ref[:, 0:1]
    ie = lax.broadcasted_iota(I32, (N_EXPERTS, ROUTE_TM), 0)
    vals, hots = [], []
    for _ in range(TOP_K):
        m = jnp.max(logits, axis=0, keepdims=True)
        idx = jnp.min(jnp.where(logits == m, ie, N_EXPERTS), axis=0, keepdims=True)
        hot = ie == idx
        vals.append(m)
        hots.append(hot)
        logits = jnp.where(hot, -jnp.inf, logits)
    exps = [jnp.exp(v - vals[0]) for v in vals]
    den = exps[0] + exps[1] + exps[2] + exps[3]

    assign = (hots[0] | hots[1] | hots[2] | hots[3]).astype(F32)
    tr = lax.broadcasted_iota(I32, (ROUTE_TM, ROUTE_TM), 0)
    tc = lax.broadcasted_iota(I32, (ROUTE_TM, ROUTE_TM), 1)
    before = (tr < tc).astype(BF16)
    prefix = jnp.dot(assign.astype(BF16), before, preferred_element_type=F32)
    count = jnp.sum(assign, axis=1, keepdims=True)
    count8 = jnp.ceil(count * (1.0 / CHUNK)) * CHUNK
    count8_b = jnp.broadcast_to(count8, (N_EXPERTS, LANES))
    er = lax.broadcasted_iota(I32, (N_EXPERTS, LANES), 0)
    ec = lax.broadcasted_iota(I32, (N_EXPERTS, LANES), 1)
    count8_pad = jnp.concatenate([count8_b, jnp.zeros((LANES - N_EXPERTS, LANES), F32)], axis=0)
    seg_start = jnp.dot((ec < er).astype(BF16), count8_pad.astype(BF16), preferred_element_type=F32)
    base = seg_start[:, 0:1] + prefix
    slots = [jnp.sum(jnp.where(hot, base, 0.0), axis=0, keepdims=True).astype(I32) for hot in hots]

    sj = lax.broadcasted_iota(I32, (ROUTE_SLOTS, ROUTE_TM), 0)
    place = (sj == slots[0]) | (sj == slots[1]) | (sj == slots[2]) | (sj == slots[3])
    z = jnp.dot(place.astype(BF16), h2.astype(BF16), preferred_element_type=F32)
    z_ref[...] = _pack_bf16_pair(z[:, :HALF], z[:, HALF:])

    zero_i = jnp.zeros((SUBLANES - TOP_K, ROUTE_TM), I32)
    slot_ref[...] = jnp.concatenate(slots + [zero_i], axis=0)
    gate_ref[...] = jnp.concatenate([e / den for e in exps] + [zero_i.astype(F32)], axis=0)

    carry = carry_ref[...]
    lane = lax.broadcasted_iota(I32, (N_EXPERTS, LANES), 1)
    meta_ref[0] = jnp.where(lane == 0, count8_b, jnp.where(lane == 1, seg_start, carry))
    carry_ref[...] = carry + count8_b


def _route_call(with_proj, x, mix, wo, vec, wrt, br):
    n, d = x.shape
    nb = n // ROUTE_TM
    blk = lambda i: (i, 0)
    fix = lambda i: (0, 0)
    return pl.pallas_call(
        functools.partial(_route_kernel, with_proj),
        grid=(nb,),
        in_specs=[
            pl.BlockSpec((ROUTE_TM, d), blk),
            pl.BlockSpec((ROUTE_TM, d), blk),
            pl.BlockSpec(wo.shape, fix),
            pl.BlockSpec((SUBLANES, d), fix),
            pl.BlockSpec(wrt.shape, fix),
            pl.BlockSpec(br.shape, fix),
        ],
        out_specs=[
            pl.BlockSpec((ROUTE_TM, d), blk),
            pl.BlockSpec((ROUTE_SLOTS, HALF), blk),
            pl.BlockSpec((SUBLANES, ROUTE_TM), lambda i: (0, i)),
            pl.BlockSpec((SUBLANES, ROUTE_TM), lambda i: (0, i)),
            pl.BlockSpec((1, N_EXPERTS, LANES), lambda i: (i, 0, 0)),
        ],
        out_shape=[
            jax.ShapeDtypeStruct((n, d), F32),
            jax.ShapeDtypeStruct((nb * ROUTE_SLOTS, HALF), U32),
            jax.ShapeDtypeStruct((SUBLANES, n), I32),
            jax.ShapeDtypeStruct((SUBLANES, n), F32),
            jax.ShapeDtypeStruct((nb, N_EXPERTS, LANES), F32),
        ],
        scratch_shapes=[pltpu.VMEM((N_EXPERTS, LANES), F32)],
        compiler_params=_params(("arbitrary",)),
        name="route_sort",
    )(x, mix, wo, vec, wrt, br)


TILE_CHUNKS = EXPERT_TM // CHUNK


def _expert_kernel(te_ref, nt_ref, nv_ref, crow_ref, z_hbm, wu_ref, bu_ref, wd_ref, bd_ref, zo_hbm,
                   xbuf, ybuf, gsem, ssem, wub_ref, wdb_ref):
    i = pl.program_id(0)
    n_tiles = nt_ref[0]
    slot = lax.rem(i, 2)
    zero_row = ROUTE_SLOTS - CHUNK

    def gather_copy(tile, s, j):
        row = crow_ref[tile * TILE_CHUNKS + j]
        row = pl.multiple_of(jnp.where(row < 0, zero_row, row), CHUNK)
        return pltpu.make_async_copy(z_hbm.at[pl.ds(row, CHUNK)],
                                     xbuf.at[s, pl.ds(j * CHUNK, CHUNK)], gsem.at[s])

    def scatter_copy(tile, s, j):
        row = pl.multiple_of(jnp.maximum(crow_ref[tile * TILE_CHUNKS + j], 0), CHUNK)
        return pltpu.make_async_copy(ybuf.at[s, pl.ds(j * CHUNK, CHUNK)],
                                     zo_hbm.at[pl.ds(row, CHUNK)], ssem.at[s])

    def start_gather(tile, s):
        for j in range(TILE_CHUNKS):
            gather_copy(tile, s, j).start()

    def for_valid_chunks(tile, fn):
        full = nv_ref[tile] == TILE_CHUNKS

        @pl.when(full)
        def _():
            for j in range(TILE_CHUNKS):
                fn(j)

        @pl.when(jnp.logical_not(full))
        def _():
            for j in range(TILE_CHUNKS):
                pl.when(j < nv_ref[tile])(functools.partial(fn, j))

    @pl.when(i < n_tiles)
    def _():
        @pl.when(i == 0)
        def _():
            start_gather(0, 0)

        @pl.when(i + 1 < n_tiles)
        def _():
            start_gather(i + 1, 1 - slot)

        for j in range(TILE_CHUNKS):
            gather_copy(i, slot, j).wait()

        @pl.when(i >= 2)
        def _():
            for_valid_chunks(i - 2, lambda j: scatter_copy(i - 2, slot, j).wait())

        @pl.when((i == 0) | (te_ref[i] != te_ref[jnp.maximum(i - 1, 0)]))
        def _():
            rows = 128

            def cast(r, _):
                sl = pl.ds(pl.multiple_of(r * rows, rows), rows)
                wub_ref[sl, :] = wu_ref[0, 0, sl, :].astype(BF16)
                wdb_ref[sl, :] = wd_ref[0, 0, sl, :].astype(BF16)
                return 0

            lax.fori_loop(0, D_MODEL // rows, cast, 0)

        lo, hi = _unpack_bf16_pair(xbuf[slot])
        h = (jnp.dot(lo, wub_ref[0:HALF, :], preferred_element_type=F32)
             + jnp.dot(hi, wub_ref[HALF:, :], preferred_element_type=F32) + bu_ref[0, 0])
        glu = jnp.minimum(h[:, :D_EXPERT], SWIGLU_LIMIT)
        lin = jnp.clip(h[:, D_EXPERT:], -SWIGLU_LIMIT, SWIGLU_LIMIT)
        act = glu * jax.nn.sigmoid(SWIGLU_ALPHA * glu) * (lin + 1.0)
        y = jnp.dot(act.astype(BF16), wdb_ref[...], preferred_element_type=F32) + bd_ref[0, 0]
        yb = y.astype(BF16).astype(F32)
        ybuf[slot] = _pack_bf16_pair(yb[:, :HALF], yb[:, HALF:])

        for_valid_chunks(i, lambda j: scatter_copy(i, slot, j).start())

        @pl.when(i == n_tiles - 1)
        def _():
            for_valid_chunks(i, lambda j: scatter_copy(i, slot, j).wait())

            @pl.when(i >= 1)
            def _():
                for_valid_chunks(i - 1, lambda j: scatter_copy(i - 1, 1 - slot, j).wait())


def _expert_call(layer, tile_expert, n_tiles, n_valid, chunk_row, z, w_up, b_up, w_down, b_down):
    max_tiles = tile_expert.shape[0]
    depth, e, d, n_up = w_up.shape

    def w_map(i, te, nt, nv, crow):
        return (layer, te[i], 0, 0)

    hbm = pl.BlockSpec(memory_space=pl.ANY)
    return pl.pallas_call(
        _expert_kernel,
        grid_spec=pltpu.PrefetchScalarGridSpec(
            num_scalar_prefetch=4,
            grid=(max_tiles,),
            in_specs=[
                hbm,
                pl.BlockSpec((1, 1, d, n_up), w_map),
                pl.BlockSpec((1, 1, 1, n_up), w_map),
                pl.BlockSpec((1, 1, D_EXPERT, d), w_map),
                pl.BlockSpec((1, 1, 1, d), w_map),
            ],
            out_specs=hbm,
            scratch_shapes=[
                pltpu.VMEM((2, EXPERT_TM, HALF), U32),
                pltpu.VMEM((2, EXPERT_TM, HALF), U32),
                pltpu.SemaphoreType.DMA((2,)),
                pltpu.SemaphoreType.DMA((2,)),
                pltpu.VMEM((d, n_up), BF16),
                pltpu.VMEM((D_EXPERT, d), BF16),
            ],
        ),
        out_shape=jax.ShapeDtypeStruct(z.shape, z.dtype),
        input_output_aliases={4: 0},
        compiler_params=_params(("arbitrary",)),
        name="expert_mlp",
    )(tile_expert, n_tiles, n_valid, chunk_row, z, w_up, b_up.reshape(depth, e, 1, n_up),
      w_down, b_down.reshape(depth, e, 1, d))


def _combine_kernel(zy_ref, slot_ref, gate_ref, xn_ref, vec_ref, o_ref):
    lo, hi = _unpack_bf16_pair(zy_ref[...])
    sj = lax.broadcasted_iota(I32, (ROUTE_TM, ROUTE_SLOTS), 1)
    slot = slot_ref[...]
    gate = gate_ref[...]
    g = jnp.zeros((ROUTE_TM, ROUTE_SLOTS), F32)
    for k in range(TOP_K):
        g = g + jnp.where(sj == slot[:, k:k + 1], gate[:, k:k + 1], 0.0)
    gb = g.astype(BF16)
    f = jnp.concatenate([jnp.dot(gb, lo, preferred_element_type=F32),
                         jnp.dot(gb, hi, preferred_element_type=F32)], axis=1)
    o_ref[...] = xn_ref[...] + vec_ref[1:2] * (_rms(f) * vec_ref[0:1])


def _combine_call(zy, slot_t, gate_t, xn, vec):
    n, d = xn.shape
    blk = lambda i: (i, 0)
    return pl.pallas_call(
        _combine_kernel,
        grid=(n // ROUTE_TM,),
        in_specs=[
            pl.BlockSpec((ROUTE_SLOTS, HALF), blk),
            pl.BlockSpec((ROUTE_TM, SUBLANES), blk),
            pl.BlockSpec((ROUTE_TM, SUBLANES), blk),
            pl.BlockSpec((ROUTE_TM, d), blk),
            pl.BlockSpec((SUBLANES, d), lambda i: (0, 0)),
        ],
        out_specs=pl.BlockSpec((ROUTE_TM, d), blk),
        out_shape=jax.ShapeDtypeStruct((n, d), F32),
        compiler_params=_params(("arbitrary",)),
        name="combine",
    )(zy, slot_t, gate_t, xn, vec)


def _vec(*rows):
    v = jnp.stack([r.astype(F32) for r in rows])
    return jnp.pad(v, ((0, SUBLANES - v.shape[0]), (0, 0)))


def _routing_tables(meta, n):
    nb = n // ROUTE_TM
    count8 = meta[:, :, 0].astype(I32)
    seg_start = meta[:, :, 1].astype(I32)
    carry = meta[:, :, 2].astype(I32)
    total = jnp.sum(count8, axis=0)
    region = (total + EXPERT_TM - 1) // EXPERT_TM * EXPERT_TM
    region_end = jnp.cumsum(region)
    offs = region_end - region
    max_rows = n * TOP_K + nb * N_EXPERTS * (CHUNK - 1) + N_EXPERTS * (EXPERT_TM - CHUNK)
    max_tiles = -(-max_rows // EXPERT_TM)
    n_tiles = (region_end[-1] // EXPERT_TM).astype(I32)
    tile_ids = jnp.minimum(jnp.arange(max_tiles, dtype=I32), n_tiles - 1)
    te = jnp.sum((region_end[None, :] // EXPERT_TM <= tile_ids[:, None]).astype(I32), axis=1)
    te = jnp.minimum(te, N_EXPERTS - 1)

    dst = jnp.concatenate([(offs[None, :] + carry).T, (offs + total)[:, None]], axis=1).reshape(-1)
    src = jnp.arange(nb, dtype=I32)[:, None] * ROUTE_SLOTS + seg_start
    shift = jnp.concatenate([src.T, jnp.zeros((N_EXPERTS, 1), I32)], axis=1).reshape(-1) - dst
    real = jnp.concatenate([jnp.ones((N_EXPERTS, nb), I32), jnp.zeros((N_EXPERTS, 1), I32)], axis=1).reshape(-1)
    d_shift = jnp.diff(shift, prepend=0)
    d_real = jnp.diff(real, prepend=0)
    rows = jnp.arange(max_tiles * TILE_CHUNKS, dtype=I32) * CHUNK
    started = dst[None, :] <= rows[:, None]
    row_shift = jnp.sum(jnp.where(started, d_shift[None, :], 0), axis=1)
    row_real = jnp.sum(jnp.where(started, d_real[None, :], 0), axis=1)
    chunk_row = jnp.where(row_real > 0, rows + row_shift, -1)
    n_valid = jnp.sum((chunk_row >= 0).astype(I32).reshape(max_tiles, TILE_CHUNKS), axis=1)
    return te, n_tiles.reshape(1), n_valid, chunk_row


def _moe_layer(layer, x, mix, wo, with_proj, route_vec, combine_vec, w_router, b_router,
               w_up, b_up, w_down, b_down):
    n = x.shape[0]
    br = jnp.broadcast_to(b_router.astype(F32)[:, None], (N_EXPERTS, LANES))
    xn, z, slots, gates, meta = _route_call(with_proj, x, mix, wo, route_vec, w_router.T, br)
    te, n_tiles, n_valid, chunk_row = _routing_tables(meta, n)
    zy = _expert_call(layer, te, n_tiles, n_valid, chunk_row, z, w_up, b_up, w_down, b_down)
    return _combine_call(zy, slots.T, gates.T, xn, combine_vec)


def kernel(x, c, ctx, c_ctx, ada_w, ada_b, norm_g, na_w_qkv, na_w_o, na_rpb, pool_w, pool_b, pool_scale,
           moe_w_router, moe_b_router, moe_w_up, moe_b_up, moe_w_down, moe_b_down):
    b, n, d = x.shape
    assert b == 1 and d == D_MODEL and ada_w.shape[0] == 2 and n % (GRID_W * ATT_Q_ROWS) == 0
    x2 = x.reshape(n, d)
    ctx2 = ctx.reshape(ctx.shape[1], d)
    rows = n // GRID_W

    cc = jnp.zeros((SUBLANES, d), F32).at[0].set(c[0]).at[1].set(c_ctx)
    mod = _ada_call(cc, ada_w, ada_b)

    def six(layer, who):
        return [mod[layer, who, j * d:(j + 1) * d] for j in range(6)]

    sh1, sc1, g1, sh2, sc2, g2 = six(0, 0)
    csh1, csc1 = six(0, 1)[:2]
    q_scale = jnp.concatenate([jnp.full((d,), HEAD_DIM ** -0.5, F32), jnp.ones((2 * d,), F32)])
    wqkv = (na_w_qkv[0] * q_scale).astype(BF16)
    qkv = _qkv_call(x2, _vec(norm_g[0, 0], sh1, sc1), wqkv, 512)
    qkv_ctx = _qkv_call(ctx2, _vec(norm_g[0, 0], csh1, csc1), wqkv, ctx2.shape[0])
    att = _attn_call(qkv, qkv_ctx, _attn_bias_table(na_rpb[0], rows))
    x2 = _moe_layer(0, x2, att, na_w_o[0].astype(BF16), True,
                    _vec(norm_g[0, 1], g1, norm_g[0, 2], sh2, sc2), _vec(norm_g[0, 3], g2),
                    moe_w_router[0], moe_b_router[0], moe_w_up, moe_b_up, moe_w_down, moe_b_down)

    sh1, sc1, g1, sh2, sc2, g2 = six(1, 0)
    mix = _pool_call(x2, _vec(norm_g[1, 0], sh1, sc1, pool_scale[0]), pool_w[0].astype(BF16),
                     jnp.pad(pool_b[0], ((0, SUBLANES - len(POOL_SIZES)), (0, 0))), 512)
    x2 = _moe_layer(1, x2, mix, jnp.zeros((SUBLANES, LANES), BF16), False,
                    _vec(norm_g[1, 1], g1, norm_g[1, 2], sh2, sc2), _vec(norm_g[1, 3], g2),
                    moe_w_router[1], moe_b_router[1], moe_w_up, moe_b_up, moe_w_down, moe_b_down)
    return x2.reshape(b, n, d)
```
